```python
import jax, jax.numpy as jnp
from jax import lax
import numpy as np

D_MODEL = 1024
BATCH = 32
SEQ = 2048
DEPTH = 1

D_MIX = D_MODEL
D_CONV = D_MIX // 2
D_ATTN = D_MIX - D_CONV
HEAD_DIM = 64
N_HEADS = D_ATTN // HEAD_DIM
CONV_WIDTH = 31
PATTERNS = ((128, 1), (512, 4), (2048, 16))
BLOCK_Q = 128
N_GROUPS = 4
EXPERTS_PER_GROUP = 8
N_EXPERTS = N_GROUPS * EXPERTS_PER_GROUP
TOP_K_EXPERT = 2
D_EXPERT = 256
EXPERT_BLOCK = 128
EPS = 1e-6
NEG_INF = -1e30

kernel_name = "hymba_conformer_dilated_hmoe_block"


def rms_norm(x, g):
    xf = x.astype(jnp.float32)
    r = lax.rsqrt(jnp.mean(xf * xf, axis=-1, keepdims=True) + EPS)
    return (xf * r * g.astype(jnp.float32)).astype(x.dtype)


def layer_norm(x, g, b):
    xf = x.astype(jnp.float32)
    mu = jnp.mean(xf, axis=-1, keepdims=True)
    var = jnp.mean(jnp.square(xf - mu), axis=-1, keepdims=True)
    y = (xf - mu) * lax.rsqrt(var + EPS) * g.astype(jnp.float32) + b.astype(jnp.float32)
    return y.astype(x.dtype)


def conformer_conv_group(a, gate, dw_w, dw_b, ln_g, ln_b):
    u = a * jax.nn.sigmoid(gate)
    c = u.shape[-1]
    y = lax.conv_general_dilated(
        u, dw_w.astype(u.dtype)[:, None, :], window_strides=(1,),
        padding=[(CONV_WIDTH - 1, 0)], dimension_numbers=("NWC", "WIO", "NWC"),
        feature_group_count=c)
    y = y + dw_b.astype(y.dtype)
    y = layer_norm(y, ln_g, ln_b)
    return jax.nn.silu(y)


def dilated_branch(q, k, v, window, dilation):
    b_, h_, s_, hd = q.shape
    d = dilation
    length = s_ // d
    w_sub = window // d
    nb = -(-length // BLOCK_Q)
    lp = nb * BLOCK_Q

    def to_strided(t):
        t = t.reshape(b_, h_, length, d, hd).transpose(0, 1, 3, 2, 4)
        t = jnp.pad(t, ((0, 0), (0, 0), (0, 0), (0, lp - length), (0, 0)))
        return t.reshape(b_, h_, d, nb, BLOCK_Q, hd)

    def with_prev(t):
        prev = jnp.pad(t[:, :, :, :-1], ((0, 0), (0, 0), (0, 0), (1, 0), (0, 0), (0, 0)))
        return jnp.concatenate([prev, t], axis=-2)

    qs = to_strided(q).astype(jnp.float32)
    kb = with_prev(to_strided(k)).astype(jnp.float32)
    vb = with_prev(to_strided(v)).astype(jnp.float32)

    s = jnp.einsum("bhrnqc,bhrnkc->bhrnqk", qs, kb) * (hd ** -0.5)
    qi = jnp.arange(BLOCK_Q)[:, None]
    ki = jnp.arange(2 * BLOCK_Q)[None, :]
    dist = BLOCK_Q + qi - ki
    band = (dist >= 0) & (dist <= w_sub)
    first = (jnp.arange(nb) == 0)[:, None, None]
    valid = band[None] & ~(first & (ki < BLOCK_Q)[None])
    s = jnp.where(valid[None, None, None], s, NEG_INF)
    m = jnp.max(s, axis=-1, keepdims=True)
    p = jnp.exp(s - m)
    den = jnp.sum(p, axis=-1, keepdims=True)
    o = jnp.einsum("bhrnqk,bhrnkc->bhrnqc", p, vb) / den
    lse = (m + jnp.log(den))[..., 0]

    o = o.reshape(b_, h_, d, lp, hd)[:, :, :, :length]
    o = o.transpose(0, 1, 3, 2, 4).reshape(b_, h_, s_, hd)
    lse = lse.reshape(b_, h_, d, lp)[..., :length].transpose(0, 1, 3, 2).reshape(b_, h_, s_)
    return o, lse


def dilated_attention_group(q, k, v):
    b_, s_, _ = q.shape
    split = lambda t: t.reshape(b_, s_, N_HEADS, HEAD_DIM).transpose(0, 2, 1, 3)
    qh, kh, vh = split(q), split(k), split(v)
    outs, lses = [], []
    for window, dilation in PATTERNS:
        o, l = dilated_branch(qh, kh, vh, window, dilation)
        outs.append(o)
        lses.append(l)
    wts = jax.nn.softmax(jnp.stack(lses, axis=0), axis=0)
    o = jnp.sum(wts[..., None] * jnp.stack(outs, axis=0), axis=0)
    return o.transpose(0, 2, 1, 3).reshape(b_, s_, D_ATTN).astype(q.dtype)


def hierarchical_moe(xn, router_group, router_expert, w_gate, w_up, w_down):
    t_ = xn.shape[0]
    g_prob = jax.nn.softmax((xn @ router_group).astype(jnp.float32), axis=-1)
    p_top, g_top = lax.top_k(g_prob, 1)
    e_logits = jnp.einsum("td,dge->tge", xn, router_expert).astype(jnp.float32)
    e_sel = jnp.take_along_axis(e_logits, g_top[:, :, None], axis=1)[:, 0]
    e_top, e_idx = lax.top_k(e_sel, TOP_K_EXPERT)
    weights = jax.nn.softmax(e_top, axis=-1) * p_top
    expert = (g_top * EXPERTS_PER_GROUP + e_idx).astype(jnp.int32)

    n_assign = t_ * TOP_K_EXPERT
    e_flat = expert.reshape(-1)
    w_flat = weights.reshape(-1)
    tok_flat = jnp.repeat(jnp.arange(t_, dtype=jnp.int32), TOP_K_EXPERT)
    order = jnp.argsort(e_flat)
    se, stok, sw = e_flat[order], tok_flat[order], w_flat[order]
    counts = jnp.zeros((N_EXPERTS,), jnp.int32).at[e_flat].add(1)
    starts = jnp.cumsum(counts) - counts
    pcounts = (counts + EXPERT_BLOCK - 1) // EXPERT_BLOCK * EXPERT_BLOCK
    pends = jnp.cumsum(pcounts)
    pstarts = pends - pcounts
    dest = pstarts[se] + (jnp.arange(n_assign, dtype=jnp.int32) - starts[se])
    n_blk = -(-n_assign // EXPERT_BLOCK) + N_EXPERTS
    n_rows = n_blk * EXPERT_BLOCK
    buf_tok = jnp.zeros((n_rows,), jnp.int32).at[dest].set(stok)
    buf_w = jnp.zeros((n_rows,), xn.dtype).at[dest].set(sw.astype(xn.dtype))
    blk_start = jnp.arange(n_blk, dtype=jnp.int32) * EXPERT_BLOCK
    blk_expert = jnp.minimum(jnp.searchsorted(pends, blk_start, side="right"),
                             N_EXPERTS - 1).astype(jnp.int32)

    def run_block(args):
        tok_b, w_b, e = args
        xb = xn[tok_b]
        hdn = jax.nn.silu(xb @ w_gate[e]) * (xb @ w_up[e])
        return (hdn @ w_down[e]) * w_b[:, None]

    out = lax.map(run_block, (buf_tok.reshape(n_blk, EXPERT_BLOCK),
                              buf_w.reshape(n_blk, EXPERT_BLOCK), blk_expert))
    return jnp.zeros_like(xn).at[buf_tok].add(out.reshape(n_rows, xn.shape[-1]))


def setup_inputs(seed: int = 0) -> dict:
    key = jax.random.key(seed)
    ks = jax.random.split(key, 16)
    f32 = jnp.float32
    nrm = lambda k, shape, scale: jax.random.normal(k, shape, f32) * scale
    d_in = 2 * D_CONV + 3 * D_ATTN
    return {
        "x": jax.random.normal(ks[0], (BATCH, SEQ, D_MODEL), f32),
        "norm_mix_g": 1.0 + nrm(ks[1], (DEPTH, D_MODEL), 0.02),
        "w_in": nrm(ks[2], (DEPTH, D_MODEL, d_in), D_MODEL ** -0.5),
        "conv_dw_w": nrm(ks[3], (DEPTH, CONV_WIDTH, D_CONV), CONV_WIDTH ** -0.5),
        "conv_dw_b": nrm(ks[4], (DEPTH, D_CONV), 0.02),
        "conv_ln_g": 1.0 + nrm(ks[5], (DEPTH, D_CONV), 0.02),
        "conv_ln_b": nrm(ks[6], (DEPTH, D_CONV), 0.02),
        "w_out": nrm(ks[7], (DEPTH, D_MIX, D_MODEL), D_MIX ** -0.5),
        "norm_ffn_g": 1.0 + nrm(ks[8], (DEPTH, D_MODEL), 0.02),
        "router_group": nrm(ks[9], (DEPTH, D_MODEL, N_GROUPS), D_MODEL ** -0.5),
        "router_expert": nrm(ks[10], (DEPTH, D_MODEL, N_GROUPS, EXPERTS_PER_GROUP), D_MODEL ** -0.5),
        "expert_w_gate": nrm(ks[11], (DEPTH, N_EXPERTS, D_MODEL, D_EXPERT), D_MODEL ** -0.5),
        "expert_w_up": nrm(ks[12], (DEPTH, N_EXPERTS, D_MODEL, D_EXPERT), D_MODEL ** -0.5),
        "expert_w_down": nrm(ks[13], (DEPTH, N_EXPERTS, D_EXPERT, D_MODEL), D_EXPERT ** -0.5),
        "norm_final_g": 1.0 + nrm(ks[14], (D_MODEL,), 0.02),
    }


def reference(x, norm_mix_g, w_in, conv_dw_w, conv_dw_b, conv_ln_g, conv_ln_b, w_out,
              norm_ffn_g, router_group, router_expert, expert_w_gate, expert_w_up,
              expert_w_down, norm_final_g):
    b_, s_, d_ = x.shape
    splits = [D_CONV, 2 * D_CONV, 2 * D_CONV + D_ATTN, 2 * D_CONV + 2 * D_ATTN]
    for layer in range(DEPTH):
        h = rms_norm(x, norm_mix_g[layer])
        proj = h @ w_in[layer]
        a, gate, q, k, v = jnp.split(proj, splits, axis=-1)
        conv_out = conformer_conv_group(a, gate, conv_dw_w[layer], conv_dw_b[layer],
                                        conv_ln_g[layer], conv_ln_b[layer])
        attn_out = dilated_attention_group(q, k, v)
        mixed = jnp.concatenate([conv_out, attn_out], axis=-1) @ w_out[layer]
        x = x + mixed
        hn = rms_norm(x, norm_ffn_g[layer]).reshape(b_ * s_, d_)
        ffn = hierarchical_moe(hn, router_group[layer], router_expert[layer],
                               expert_w_gate[layer], expert_w_up[layer], expert_w_down[layer])
        x = x + ffn.reshape(b_, s_, d_)
    return rms_norm(x, norm_final_g)
```

```python
import functools

import jax
import jax.numpy as jnp
from jax import lax
from jax.experimental import pallas as pl
from jax.experimental.pallas import tpu as pltpu

F32 = jnp.float32
BF16 = jnp.bfloat16

D_MODEL = 1024
D_CONV = 512
D_ATTN = 512
HEAD_DIM = 64
CONV_WIDTH = 31
SEQ = 2048
DIL = 16
N_RES = SEQ // DIL
BQ = 128
N_GROUPS = 4
EXPERTS_PER_GROUP = 8
N_EXPERTS = N_GROUPS * EXPERTS_PER_GROUP
D_EXPERT = 256
EPS = 1e-6
NEG_INF = -1e30

LANES = 128
ROW_TILES = D_MODEL // LANES

TS = 512
HALO = 32
CONV_ROWS = 64
TM3 = 512
TM4 = 256
TM6 = 256
BLK = 256
ROUTER_LANE0 = N_GROUPS
VMEM_LIMIT = 48 * 1024 * 1024


def _params(sem):
    return pltpu.CompilerParams(dimension_semantics=sem, vmem_limit_bytes=VMEM_LIMIT)


def _inproj_kernel(x_ref, g_ref, wag_ref, wqkv_ref, dww_ref, dwb_ref, lng_ref, lnb_ref,
                   conv_ref, qn_ref, kn_ref, vn_ref, qp_ref, kp_ref, vp_ref,
                   halo_ref, qkv_ref):
    s = pl.program_id(1)
    x = x_ref[0]
    r = lax.rsqrt(jnp.mean(x * x, axis=-1, keepdims=True) + EPS)
    h = (x * r * g_ref[...]).astype(BF16)

    ag = jnp.dot(h, wag_ref[...], preferred_element_type=F32)
    u = ag[:, :D_CONV] * jax.nn.sigmoid(ag[:, D_CONV:])

    @pl.when(s == 0)
    def _():
        halo_ref[0:HALO, :] = jnp.zeros((HALO, D_CONV), F32)

    halo_ref[HALO:HALO + TS, :] = u
    dwb = dwb_ref[...]
    lng = lng_ref[...]
    lnb = lnb_ref[...]
    for c in range(TS // CONV_ROWS):
        r0 = c * CONV_ROWS
        acc = jnp.zeros((CONV_ROWS, D_CONV), F32)
        for k in range(CONV_WIDTH):
            off = HALO - (CONV_WIDTH - 1) + k + r0
            acc = acc + dww_ref[k:k + 1, :] * halo_ref[off:off + CONV_ROWS, :]
        y = acc + dwb
        mu = jnp.mean(y, axis=-1, keepdims=True)
        yc = y - mu
        var = jnp.mean(yc * yc, axis=-1, keepdims=True)
        yn = yc * lax.rsqrt(var + EPS) * lng + lnb
        conv_ref[0, r0:r0 + CONV_ROWS, :] = (yn * jax.nn.sigmoid(yn)).astype(BF16)
    halo_ref[0:HALO, :] = halo_ref[TS:TS + HALO, :]

    qkv = jnp.dot(h, wqkv_ref[...], preferred_element_type=F32)
    qn_ref[0] = qkv[:, 0:D_ATTN].astype(BF16)
    kn_ref[0] = qkv[:, D_ATTN:2 * D_ATTN].astype(BF16)
    vn_ref[0] = qkv[:, 2 * D_ATTN:3 * D_ATTN].astype(BF16)
    n_slabs = 3 * D_ATTN // LANES
    for c in range(n_slabs):
        qkv_ref[c] = qkv[:, c * LANES:(c + 1) * LANES]
    nj = TS // DIL
    per = D_ATTN // LANES
    for res in range(DIL):
        rows = [qkv_ref[c, pl.ds(res, nj, stride=DIL), :].astype(BF16) for c in range(n_slabs)]
        qp_ref[0, res] = jnp.concatenate(rows[0:per], axis=1)
        kp_ref[0, res] = jnp.concatenate(rows[per:2 * per], axis=1)
        vp_ref[0, res] = jnp.concatenate(rows[2 * per:3 * per], axis=1)


def _inproj(x, g, w_ag, w_qkv, dww, dwb, lng, lnb):
    b = x.shape[0]
    const = lambda *shape: pl.BlockSpec(shape, lambda i, j: (0,) * len(shape))
    nat = pl.BlockSpec((1, TS, D_ATTN), lambda i, j: (i, j, 0))
    perm = pl.BlockSpec((1, DIL, TS // DIL, D_ATTN), lambda i, j: (i, 0, j, 0))
    nat_shape = jax.ShapeDtypeStruct((b, SEQ, D_ATTN), BF16)
    perm_shape = jax.ShapeDtypeStruct((b, DIL, N_RES, D_ATTN), BF16)
    return pl.pallas_call(
        _inproj_kernel,
        grid=(b, SEQ // TS),
        in_specs=[
            pl.BlockSpec((1, TS, D_MODEL), lambda i, j: (i, j, 0)),
            const(1, D_MODEL),
            const(D_MODEL, 2 * D_CONV),
            const(D_MODEL, 3 * D_ATTN),
            const(CONV_WIDTH, D_CONV),
            const(1, D_CONV),
            const(1, D_CONV),
            const(1, D_CONV),
        ],
        out_specs=[nat, nat, nat, nat, perm, perm, perm],
        out_shape=[nat_shape, nat_shape, nat_shape, nat_shape, perm_shape, perm_shape, perm_shape],
        scratch_shapes=[
            pltpu.VMEM((HALO + TS, D_CONV), F32),
            pltpu.VMEM((3 * D_ATTN // LANES, TS, LANES), F32),
        ],
        compiler_params=_params(("arbitrary", "arbitrary")),
        name="inproj_conv",
    )(x, g, w_ag, w_qkv, dww, dwb, lng, lnb)


MASK_A, MASK_A_FIRST, MASK_B, MASK_B_FIRST, MASK_C = range(5)


def _band_masks():
    row = lax.broadcasted_iota(jnp.int32, (BQ, 2 * BQ), 0)
    col = lax.broadcasted_iota(jnp.int32, (BQ, 2 * BQ), 1)
    cur = col >= BQ
    kcol = jnp.where(cur, col - BQ, col)
    dist_a = row - kcol + jnp.where(cur, 0, BQ)
    ok_a = (dist_a >= 0) & (dist_a <= BQ)
    sub = N_RES // 4
    pos_q = 4 * (row % sub) + row // sub
    pos_k = 4 * (kcol % sub) + kcol // sub
    dist_b = pos_q - pos_k + jnp.where(cur, 0, BQ)
    ok_b = (dist_b >= 0) & (dist_b <= BQ)
    ok_c = (row >= col) & (col < BQ)
    one = lambda m: jnp.where(m, 1.0, 0.0).astype(F32)
    return [one(ok_a), one(ok_a & cur), one(ok_b), one(ok_b & cur), one(ok_c)]


def _attn_kernel(qn_ref, kn_ref, vn_ref, qp_ref, kp_ref, vp_ref, o_ref,
                 mask_ref, oa_ref, la_ref, ob_ref, lb_ref, oc_ref, lc_ref, on_ref):
    @pl.when((pl.program_id(0) == 0) & (pl.program_id(1) == 0))
    def _():
        for idx, m in enumerate(_band_masks()):
            mask_ref[idx] = m

    lane = lax.broadcasted_iota(jnp.int32, (BQ, LANES), 1)
    head0 = lane < HEAD_DIM
    scale = HEAD_DIM ** -0.5

    def block(q, k, v, valid):
        zero = jnp.zeros_like(q)
        qs = q * jnp.asarray(scale, q.dtype)
        qq = jnp.concatenate([jnp.where(head0, qs, zero), jnp.where(head0, zero, qs)], axis=0)
        sc = lax.dot_general(qq, k, (((1,), (1,)), ((), ())), preferred_element_type=F32)
        sc = jnp.where(jnp.concatenate([valid, valid], axis=0), sc, NEG_INF)
        m = jnp.max(sc, axis=-1, keepdims=True)
        p = jnp.exp(sc - m)
        den = jnp.sum(p, axis=-1, keepdims=True)
        o = jnp.dot(p.astype(BF16), v, preferred_element_type=F32) / den
        lse = m + jnp.log(den)
        return (jnp.where(head0, o[:BQ], o[BQ:]),
                jnp.where(head0, lse[:BQ], lse[BQ:]))

    def body_a(n, carry):
        r0 = pl.multiple_of(n * BQ, BQ)
        rp = pl.multiple_of(jnp.maximum(n - 1, 0) * BQ, BQ)
        q = qn_ref[0, pl.ds(r0, BQ), :]
        k = jnp.concatenate([kn_ref[0, pl.ds(rp, BQ), :], kn_ref[0, pl.ds(r0, BQ), :]], axis=0)
        v = jnp.concatenate([vn_ref[0, pl.ds(rp, BQ), :], vn_ref[0, pl.ds(r0, BQ), :]], axis=0)
        valid = mask_ref[jnp.where(n == 0, MASK_A_FIRST, MASK_A)] > 0.5
        o, l = block(q, k, v, valid)
        oa_ref[pl.ds(r0, BQ), :] = o
        la_ref[pl.ds(r0, BQ), :] = l
        return carry

    lax.fori_loop(0, SEQ // BQ, body_a, 0)

    sub = N_RES // 4
    for r4 in range(4):
        def body_b(n, carry, r4=r4):
            c0 = pl.multiple_of(n * sub, sub)
            cp = pl.multiple_of(jnp.maximum(n - 1, 0) * sub, sub)
            starts = [(4 * a + r4) * N_RES for a in range(4)]
            q = jnp.concatenate([qp_ref[0, pl.ds(st + c0, sub), :] for st in starts], axis=0)
            k = jnp.concatenate([kp_ref[0, pl.ds(st + cp, sub), :] for st in starts]
                                + [kp_ref[0, pl.ds(st + c0, sub), :] for st in starts], axis=0)
            v = jnp.concatenate([vp_ref[0, pl.ds(st + cp, sub), :] for st in starts]
                                + [vp_ref[0, pl.ds(st + c0, sub), :] for st in starts], axis=0)
            valid = mask_ref[jnp.where(n == 0, MASK_B_FIRST, MASK_B)] > 0.5
            o, l = block(q, k, v, valid)
            for a, st in enumerate(starts):
                ob_ref[pl.ds(st + c0, sub), :] = o[a * sub:(a + 1) * sub]
                lb_ref[pl.ds(st + c0, sub), :] = l[a * sub:(a + 1) * sub]
            return carry

        lax.fori_loop(0, 4, body_b, 0)

    def body_c(res, carry):
        r0 = pl.multiple_of(res * N_RES, N_RES)
        q = qp_ref[0, pl.ds(r0, N_RES), :]
        k = kp_ref[0, pl.ds(r0, N_RES), :]
        v = vp_ref[0, pl.ds(r0, N_RES), :]
        valid = mask_ref[MASK_C][:, :BQ] > 0.5
        o, l = block(q, k, v, valid)
        oc_ref[pl.ds(r0, N_RES), :] = o
        lc_ref[pl.ds(r0, N_RES), :] = l
        return carry

    lax.fori_loop(0, DIL, body_c, 0)

    for res in range(DIL):
        nat = pl.ds(res, N_RES, stride=DIL)
        blk = pl.ds(res * N_RES, N_RES)
        la, lb, lc = la_ref[nat, :], lb_ref[blk, :], lc_ref[blk, :]
        m = jnp.maximum(jnp.maximum(la, lb), lc)
        ea, eb, ec = jnp.exp(la - m), jnp.exp(lb - m), jnp.exp(lc - m)
        out = (ea * oa_ref[nat, :] + eb * ob_ref[blk, :] + ec * oc_ref[blk, :]) / (ea + eb + ec)
        on_ref[nat, :] = out
    o_ref[0] = on_ref[...].astype(BF16)


def _attention(qn, kn, vn, qp, kp, vp):
    b = qn.shape[0]
    spec = pl.BlockSpec((1, SEQ, LANES), lambda i, j: (i, 0, j))
    buf = pltpu.VMEM((SEQ, LANES), F32)
    return pl.pallas_call(
        _attn_kernel,
        grid=(b, D_ATTN // LANES),
        in_specs=[spec] * 6,
        out_specs=spec,
        out_shape=jax.ShapeDtypeStruct((b, SEQ, D_ATTN), BF16),
        scratch_shapes=[pltpu.VMEM((5, BQ, 2 * BQ), F32)] + [buf] * 7,
        compiler_params=_params(("arbitrary", "arbitrary")),
        name="dilated_attention",
    )(qn, kn, vn, qp, kp, vp)


META_E1, META_E2, META_R1, META_R2, META_W1, META_W2 = range(6)


def _outproj_router_kernel(conv_ref, attn_ref, x_ref, wo1_ref, wo2_ref, g_ref, rt_ref,
                           x2_ref, hn_ref, meta_ref, cnt_ref,
                           tri_ref, rhi_ref, rlo_ref, run_ref):
    @pl.when(pl.program_id(0) == 0)
    def _():
        row = lax.broadcasted_iota(jnp.int32, (TM3, TM3), 0)
        col = lax.broadcasted_iota(jnp.int32, (TM3, TM3), 1)
        tri_ref[...] = jnp.where(col < row, 1.0, 0.0).astype(BF16)
        rt = rt_ref[...]
        hi = rt.astype(BF16)
        rhi_ref[...] = hi
        rlo_ref[...] = (rt - hi.astype(F32)).astype(BF16)
        run_ref[...] = jnp.zeros_like(run_ref)

    mixed = (jnp.dot(conv_ref[...], wo1_ref[...], preferred_element_type=F32)
             + jnp.dot(attn_ref[...], wo2_ref[...], preferred_element_type=F32))
    x2 = x_ref[...] + mixed
    x2_ref[...] = x2
    hn = x2 * lax.rsqrt(jnp.mean(x2 * x2, axis=-1, keepdims=True) + EPS) * g_ref[...]
    for c in range(ROW_TILES):
        hn_ref[:, c, :] = hn[:, c * LANES:(c + 1) * LANES]

    hi = hn.astype(BF16)
    lo = (hn - hi.astype(F32)).astype(BF16)
    logits = (jnp.dot(hi, rhi_ref[...], preferred_element_type=F32)
              + jnp.dot(hi, rlo_ref[...], preferred_element_type=F32)
              + jnp.dot(lo, rhi_ref[...], preferred_element_type=F32))

    lane = lax.broadcasted_iota(jnp.int32, (TM3, LANES), 1).astype(F32)
    big = jnp.float32(1e9)
    low = jnp.float32(-3e38)
    rmax = lambda v: jnp.max(v, axis=-1, keepdims=True)
    rmin = lambda v: jnp.min(v, axis=-1, keepdims=True)
    rsum = lambda v: jnp.sum(v, axis=-1, keepdims=True)

    is_g = lane < N_GROUPS
    gmax = rmax(jnp.where(is_g, logits, low))
    gidx = rmin(jnp.where(is_g & (logits == gmax), lane, big))
    p_top = 1.0 / rsum(jnp.where(is_g, jnp.exp(logits - gmax), 0.0))

    first = ROUTER_LANE0 + EXPERTS_PER_GROUP * gidx
    sel = (lane >= first) & (lane < first + EXPERTS_PER_GROUP)
    e1 = rmax(jnp.where(sel, logits, low))
    i1 = rmin(jnp.where(sel & (logits == e1), lane, big))
    sel2 = sel & (lane != i1)
    e2 = rmax(jnp.where(sel2, logits, low))
    i2 = rmin(jnp.where(sel2 & (logits == e2), lane, big))
    t = jnp.exp(e2 - e1)
    w1 = p_top / (1.0 + t)
    w2 = p_top * t / (1.0 + t)

    hit1 = lane == i1
    hit2 = lane == i2
    onehot = jnp.where(hit1 | hit2, 1.0, 0.0).astype(F32)
    before = jnp.dot(tri_ref[...], onehot.astype(BF16), preferred_element_type=F32) + run_ref[...]
    r1 = rsum(jnp.where(hit1, before, 0.0))
    r2 = rsum(jnp.where(hit2, before, 0.0))
    run_ref[...] = run_ref[...] + jnp.sum(onehot, axis=0, keepdims=True)
    cnt_ref[...] = run_ref[...]

    cols = {META_E1: i1 - ROUTER_LANE0, META_E2: i2 - ROUTER_LANE0,
            META_R1: r1, META_R2: r2, META_W1: w1, META_W2: w2}
    meta = jnp.zeros((TM3, LANES), F32)
    for idx, val in cols.items():
        meta = jnp.where(lane == idx, val, meta)
    meta_ref[...] = meta


def _outproj_router(conv, attn, x, wo1, wo2, g, rt):
    t = x.shape[0]
    const = lambda *shape: pl.BlockSpec(shape, lambda i: (0,) * len(shape))
    return pl.pallas_call(
        _outproj_router_kernel,
        grid=(t // TM3,),
        in_specs=[
            pl.BlockSpec((TM3, D_CONV), lambda i: (i, 0)),
            pl.BlockSpec((TM3, D_ATTN), lambda i: (i, 0)),
            pl.BlockSpec((TM3, D_MODEL), lambda i: (i, 0)),
            const(D_CONV, D_MODEL),
            const(D_ATTN, D_MODEL),
            const(1, D_MODEL),
            const(D_MODEL, LANES),
        ],
        out_specs=[
            pl.BlockSpec((TM3, D_MODEL), lambda i: (i, 0)),
            pl.BlockSpec((TM3, ROW_TILES, LANES), lambda i: (i, 0, 0)),
            pl.BlockSpec((TM3, LANES), lambda i: (i, 0)),
            const(1, LANES),
        ],
        out_shape=[
            jax.ShapeDtypeStruct((t, D_MODEL), F32),
            jax.ShapeDtypeStruct((t, ROW_TILES, LANES), F32),
            jax.ShapeDtypeStruct((t, LANES), F32),
            jax.ShapeDtypeStruct((1, LANES), F32),
        ],
        scratch_shapes=[
            pltpu.VMEM((TM3, TM3), BF16),
            pltpu.VMEM((D_MODEL, LANES), BF16),
            pltpu.VMEM((D_MODEL, LANES), BF16),
            pltpu.VMEM((1, LANES), F32),
        ],
        compiler_params=_params(("arbitrary",)),
        name="outproj_router",
    )(conv, attn, x, wo1, wo2, g, rt)


def _dispatch_kernel(dest_ref, cnt_ref, pstart_ref, pcnt_ref, nused_ref, hn_ref, xs_ref,
                     zero_ref, sem, zsem):
    i = pl.program_id(0)
    base = i * (2 * TM4)

    def copy_row(t, carry):
        for k in range(2):
            d = dest_ref[base + 2 * t + k]
            pltpu.make_async_copy(hn_ref.at[t], xs_ref.at[d], sem).start()
        return carry

    lax.fori_loop(0, TM4, copy_row, 0)

    @pl.when(i == 0)
    def _():
        zero_ref[...] = jnp.zeros_like(zero_ref)

        def per_expert(e, total):
            n_pad = pcnt_ref[e] - cnt_ref[e]
            first = pstart_ref[e] + cnt_ref[e]

            def fill(r, carry):
                pltpu.make_async_copy(zero_ref.at[0], xs_ref.at[first + r], zsem).start()
                return carry

            lax.fori_loop(0, n_pad, fill, 0)
            return total + n_pad

        total = lax.fori_loop(0, N_EXPERTS, per_expert, 0)

        def drain(r, carry):
            pltpu.make_async_copy(zero_ref.at[0], xs_ref.at[0], zsem).wait()
            return carry

        lax.fori_loop(0, total, drain, 0)

        n_blk = xs_ref.shape[0] // BLK
        tail = lambda blk: pltpu.make_async_copy(
            zero_ref, xs_ref.at[pl.ds(pl.multiple_of(blk * BLK, BLK), BLK)], zsem)

        def fill_block(blk, carry):
            tail(blk).start()
            return carry

        def drain_block(blk, carry):
            tail(blk).wait()
            return carry

        lax.fori_loop(nused_ref[0], n_blk, fill_block, 0)
        lax.fori_loop(nused_ref[0], n_blk, drain_block, 0)

    for _ in range(2):
        pltpu.make_async_copy(hn_ref, xs_ref.at[pl.ds(0, TM4)], sem).wait()


def _dispatch(dest, counts, pstarts, pcounts, n_used, hn3, n_rows):
    t = hn3.shape[0]
    return pl.pallas_call(
        _dispatch_kernel,
        grid_spec=pltpu.PrefetchScalarGridSpec(
            num_scalar_prefetch=5,
            grid=(t // TM4,),
            in_specs=[pl.BlockSpec((TM4, ROW_TILES, LANES), lambda i, *_: (i, 0, 0))],
            out_specs=pl.BlockSpec(memory_space=pl.ANY),
            scratch_shapes=[
                pltpu.VMEM((BLK, ROW_TILES, LANES), F32),
                pltpu.SemaphoreType.DMA,
                pltpu.SemaphoreType.DMA,
            ],
        ),
        out_shape=jax.ShapeDtypeStruct((n_rows, ROW_TILES, LANES), F32),
        compiler_params=_params(("arbitrary",)),
        name="moe_dispatch",
    )(dest, counts, pstarts, pcounts, n_used, hn3)


def _ffn_kernel(bexp_ref, nused_ref, x_ref, wg_ref, wu_ref, wd_ref, y_ref):
    b = pl.program_id(0)

    @pl.when(b < nused_ref[0])
    def _():
        xb = jnp.concatenate([x_ref[:, c, :] for c in range(ROW_TILES)], axis=1).astype(BF16)
        g = jnp.dot(xb, wg_ref[0], preferred_element_type=F32)
        u = jnp.dot(xb, wu_ref[0], preferred_element_type=F32)
        h = (g * jax.nn.sigmoid(g) * u).astype(BF16)
        y = jnp.dot(h, wd_ref[0], preferred_element_type=F32)
        for c in range(ROW_TILES):
            y_ref[:, c, :] = y[:, c * LANES:(c + 1) * LANES]

    @pl.when(b >= nused_ref[0])
    def _():
        y_ref[...] = jnp.zeros_like(y_ref)


def _expert_ffn(blk_expert, n_used, xs, wg, wu, wd):
    n_rows = xs.shape[0]
    return pl.pallas_call(
        _ffn_kernel,
        grid_spec=pltpu.PrefetchScalarGridSpec(
            num_scalar_prefetch=2,
            grid=(n_rows // BLK,),
            in_specs=[
                pl.BlockSpec((BLK, ROW_TILES, LANES),
                             lambda b, be, nu: (jnp.minimum(b, nu[0] - 1), 0, 0)),
                pl.BlockSpec((1, D_MODEL, D_EXPERT), lambda b, be, nu: (be[b], 0, 0)),
                pl.BlockSpec((1, D_MODEL, D_EXPERT), lambda b, be, nu: (be[b], 0, 0)),
                pl.BlockSpec((1, D_EXPERT, D_MODEL), lambda b, be, nu: (be[b], 0, 0)),
            ],
            out_specs=pl.BlockSpec((BLK, ROW_TILES, LANES), lambda b, be, nu: (b, 0, 0)),
        ),
        out_shape=jax.ShapeDtypeStruct((n_rows, ROW_TILES, LANES), F32),
        compiler_params=_params(("arbitrary",)),
        name="expert_ffn",
    )(blk_expert, n_used, xs, wg, wu, wd)


def _combine_kernel(dest_ref, x2_ref, meta_ref, g_ref, ys_ref, o_ref, ybuf_ref, sem):
    i = pl.program_id(0)
    base = i * (2 * TM6)

    def fetch_row(t, carry):
        for k in range(2):
            d = dest_ref[base + 2 * t + k]
            pltpu.make_async_copy(ys_ref.at[d], ybuf_ref.at[k * TM6 + t], sem).start()
        return carry

    lax.fori_loop(0, TM6, fetch_row, 0)
    pltpu.make_async_copy(ys_ref.at[pl.ds(0, 2 * TM6)], ybuf_ref, sem).wait()

    rows = lambda k: jnp.concatenate(
        [ybuf_ref[k * TM6:(k + 1) * TM6, c, :] for c in range(ROW_TILES)], axis=1)
    w1 = meta_ref[:, META_W1:META_W1 + 1]
    w2 = meta_ref[:, META_W2:META_W2 + 1]
    x = x2_ref[...] + w1 * rows(0) + w2 * rows(1)
    o_ref[...] = x * lax.rsqrt(jnp.mean(x * x, axis=-1, keepdims=True) + EPS) * g_ref[...]


def _combine(dest, x2, meta, g, ys):
    t = x2.shape[0]
    return pl.pallas_call(
        _combine_kernel,
        grid_spec=pltpu.PrefetchScalarGridSpec(
            num_scalar_prefetch=1,
            grid=(t // TM6,),
            in_specs=[
                pl.BlockSpec((TM6, D_MODEL), lambda i, *_: (i, 0)),
                pl.BlockSpec((TM6, LANES), lambda i, *_: (i, 0)),
                pl.BlockSpec((1, D_MODEL), lambda i, *_: (0, 0)),
                pl.BlockSpec(memory_space=pl.ANY),
            ],
            out_specs=pl.BlockSpec((TM6, D_MODEL), lambda i, *_: (i, 0)),
            scratch_shapes=[
                pltpu.VMEM((2 * TM6, ROW_TILES, LANES), F32),
                pltpu.SemaphoreType.DMA,
            ],
        ),
        out_shape=jax.ShapeDtypeStruct((t, D_MODEL), F32),
        compiler_params=_params(("arbitrary",)),
        name="moe_combine",
    )(dest, x2, meta, g, ys)


def _layer(x, norm_mix_g, w_in, conv_dw_w, conv_dw_b, conv_ln_g, conv_ln_b, w_out,
           norm_ffn_g, router_group, router_expert, w_gate, w_up, w_down, out_g):
    b, s, d = x.shape
    assert s == SEQ and d == D_MODEL
    t = b * s
    row = lambda v: v.reshape(1, -1).astype(F32)

    w_in_b = w_in.astype(BF16)
    conv, qn, kn, vn, qp, kp, vp = _inproj(
        x, row(norm_mix_g), w_in_b[:, :2 * D_CONV], w_in_b[:, 2 * D_CONV:],
        conv_dw_w.astype(F32), row(conv_dw_b), row(conv_ln_g), row(conv_ln_b))
    flat = lambda a: a.reshape(b, SEQ, D_ATTN)
    attn = _attention(qn, kn, vn, flat(qp), flat(kp), flat(vp))

    w_out_b = w_out.astype(BF16)
    router = jnp.concatenate(
        [router_group, router_expert.reshape(D_MODEL, N_EXPERTS),
         jnp.zeros((D_MODEL, LANES - N_GROUPS - N_EXPERTS), F32)], axis=1)
    x2, hn3, meta, cnt = _outproj_router(
        conv.reshape(t, D_CONV), attn.reshape(t, D_ATTN), x.reshape(t, D_MODEL),
        w_out_b[:D_CONV], w_out_b[D_CONV:], row(norm_ffn_g), router)

    counts = cnt[0, ROUTER_LANE0:ROUTER_LANE0 + N_EXPERTS].astype(jnp.int32)
    pcounts = (counts + BLK - 1) // BLK * BLK
    pends = jnp.cumsum(pcounts)
    pstarts = pends - pcounts
    experts = meta[:, META_E1:META_E2 + 1].astype(jnp.int32)
    ranks = meta[:, META_R1:META_R2 + 1].astype(jnp.int32)
    onehot = experts[..., None] == jnp.arange(N_EXPERTS, dtype=jnp.int32)
    dest = (jnp.sum(jnp.where(onehot, pstarts, 0), axis=-1) + ranks).reshape(-1)
    n_blk = (2 * t) // BLK + N_EXPERTS
    blk_start = jnp.arange(n_blk, dtype=jnp.int32) * BLK
    blk_expert = jnp.minimum(jnp.searchsorted(pends, blk_start, side="right"),
                             N_EXPERTS - 1).astype(jnp.int32)
    n_used = (pends[-1:] // BLK).astype(jnp.int32)

    xs = _dispatch(dest, counts, pstarts, pcounts, n_used, hn3, n_blk * BLK)
    ys = _expert_ffn(blk_expert, n_used, xs,
                     w_gate.astype(BF16), w_up.astype(BF16), w_down.astype(BF16))
    out = _combine(dest, x2, meta, row(out_g), ys)
    return out.reshape(b, s, d)


def kernel(x, norm_mix_g, w_in, conv_dw_w, conv_dw_b, conv_ln_g, conv_ln_b, w_out, norm_ffn_g,
           router_group, router_expert, expert_w_gate, expert_w_up, expert_w_down, norm_final_g):
    depth = w_in.shape[0]
    assert depth == 1
    return _layer(x, norm_mix_g[0], w_in[0], conv_dw_w[0], conv_dw_b[0], conv_ln_g[0],
                  conv_ln_b[0], w_out[0], norm_ffn_g[0], router_group[0], router_expert[0],
                  expert_w_gate[0], expert_w_up[0], expert_w_down[0], norm_final_g)
```

```python
import functools

import jax
import jax.numpy as jnp
from jax import lax
from jax.experimental import pallas as pl
from jax.experimental.pallas import tpu as pltpu

F32 = jnp.float32
BF16 = jnp.bfloat16

D_MODEL = 1024
D_CONV = 512
D_ATTN = 512
HEAD_DIM = 64
CONV_WIDTH = 31
SEQ = 2048
DIL = 16
N_RES = SEQ // DIL
BQ = 128
N_GROUPS = 4
EXPERTS_PER_GROUP = 8
N_EXPERTS = N_GROUPS * EXPERTS_PER_GROUP
D_EXPERT = 256
EPS = 1e-6
NEG_INF = -1e30

LANES = 128
ROW_TILES = D_MODEL // LANES

TS = 512
HALO = 32
CONV_ROWS = 64
TM3 = 512
TM4 = 256
TM6 = 256
BLK = 256
ROUTER_LANE0 = N_GROUPS
UNROLL = 4
VMEM_LIMIT = 48 * 1024 * 1024


def _params(sem):
    return pltpu.CompilerParams(dimension_semantics=sem, vmem_limit_bytes=VMEM_LIMIT)


def _inproj_kernel(x_ref, g_ref, wag_ref, wqkv_ref, dww_ref, dwb_ref, lng_ref, lnb_ref,
                   conv_ref, qn_ref, kn_ref, vn_ref, qp_ref, kp_ref, vp_ref,
                   halo_ref, qkv_ref):
    s = pl.program_id(1)
    x = x_ref[0]
    r = lax.rsqrt(jnp.mean(x * x, axis=-1, keepdims=True) + EPS)
    h = (x * r * g_ref[...]).astype(BF16)

    ag = jnp.dot(h, wag_ref[...], preferred_element_type=F32)
    u = ag[:, :D_CONV] * jax.nn.sigmoid(ag[:, D_CONV:])

    @pl.when(s == 0)
    def _():
        halo_ref[0:HALO, :] = jnp.zeros((HALO, D_CONV), F32)

    halo_ref[HALO:HALO + TS, :] = u
    dwb = dwb_ref[...]
    lng = lng_ref[...]
    lnb = lnb_ref[...]
    for c in range(TS // CONV_ROWS):
        r0 = c * CONV_ROWS
        acc = jnp.zeros((CONV_ROWS, D_CONV), F32)
        for k in range(CONV_WIDTH):
            off = HALO - (CONV_WIDTH - 1) + k + r0
            acc = acc + dww_ref[k:k + 1, :] * halo_ref[off:off + CONV_ROWS, :]
        y = acc + dwb
        mu = jnp.mean(y, axis=-1, keepdims=True)
        yc = y - mu
        var = jnp.mean(yc * yc, axis=-1, keepdims=True)
        yn = yc * lax.rsqrt(var + EPS) * lng + lnb
        conv_ref[0, r0:r0 + CONV_ROWS, :] = (yn * jax.nn.sigmoid(yn)).astype(BF16)
    halo_ref[0:HALO, :] = halo_ref[TS:TS + HALO, :]

    qkv = jnp.dot(h, wqkv_ref[...], preferred_element_type=F32)
    qn_ref[0] = qkv[:, 0:D_ATTN].astype(BF16)
    kn_ref[0] = qkv[:, D_ATTN:2 * D_ATTN].astype(BF16)
    vn_ref[0] = qkv[:, 2 * D_ATTN:3 * D_ATTN].astype(BF16)
    n_slabs = 3 * D_ATTN // LANES
    for c in range(n_slabs):
        qkv_ref[c] = qkv[:, c * LANES:(c + 1) * LANES]
    nj = TS // DIL
    per = D_ATTN // LANES
    for res in range(DIL):
        rows = [qkv_ref[c, pl.ds(res, nj, stride=DIL), :].astype(BF16) for c in range(n_slabs)]
        qp_ref[0, res] = jnp.concatenate(rows[0:per], axis=1)
        kp_ref[0, res] = jnp.concatenate(rows[per:2 * per], axis=1)
        vp_ref[0, res] = jnp.concatenate(rows[2 * per:3 * per], axis=1)


def _inproj(x, g, w_ag, w_qkv, dww, dwb, lng, lnb):
    b = x.shape[0]
    const = lambda *shape: pl.BlockSpec(shape, lambda i, j: (0,) * len(shape))
    nat = pl.BlockSpec((1, TS, D_ATTN), lambda i, j: (i, j, 0))
    perm = pl.BlockSpec((1, DIL, TS // DIL, D_ATTN), lambda i, j: (i, 0, j, 0))
    nat_shape = jax.ShapeDtypeStruct((b, SEQ, D_ATTN), BF16)
    perm_shape = jax.ShapeDtypeStruct((b, DIL, N_RES, D_ATTN), BF16)
    return pl.pallas_call(
        _inproj_kernel,
        grid=(b, SEQ // TS),
        in_specs=[
            pl.BlockSpec((1, TS, D_MODEL), lambda i, j: (i, j, 0)),
            const(1, D_MODEL),
            const(D_MODEL, 2 * D_CONV),
            const(D_MODEL, 3 * D_ATTN),
            const(CONV_WIDTH, D_CONV),
            const(1, D_CONV),
            const(1, D_CONV),
            const(1, D_CONV),
        ],
        out_specs=[nat, nat, nat, nat, perm, perm, perm],
        out_shape=[nat_shape, nat_shape, nat_shape, nat_shape, perm_shape, perm_shape, perm_shape],
        scratch_shapes=[
            pltpu.VMEM((HALO + TS, D_CONV), F32),
            pltpu.VMEM((3 * D_ATTN // LANES, TS, LANES), F32),
        ],
        compiler_params=_params(("arbitrary", "arbitrary")),
        name="inproj_conv",
    )(x, g, w_ag, w_qkv, dww, dwb, lng, lnb)


MASK_A, MASK_A_FIRST, MASK_B, MASK_B_FIRST, MASK_C = range(5)


def _band_masks():
    row = lax.broadcasted_iota(jnp.int32, (BQ, 2 * BQ), 0)
    col = lax.broadcasted_iota(jnp.int32, (BQ, 2 * BQ), 1)
    cur = col >= BQ
    kcol = jnp.where(cur, col - BQ, col)
    dist_a = row - kcol + jnp.where(cur, 0, BQ)
    ok_a = (dist_a >= 0) & (dist_a <= BQ)
    sub = N_RES // 4
    pos_q = 4 * (row % sub) + row // sub
    pos_k = 4 * (kcol % sub) + kcol // sub
    dist_b = pos_q - pos_k + jnp.where(cur, 0, BQ)
    ok_b = (dist_b >= 0) & (dist_b <= BQ)
    ok_c = (row >= col) & (col < BQ)
    one = lambda m: jnp.where(m, 1.0, 0.0).astype(F32)
    return [one(ok_a), one(ok_a & cur), one(ok_b), one(ok_b & cur), one(ok_c)]


def _attn_kernel(qn_ref, kn_ref, vn_ref, qp_ref, kp_ref, vp_ref, o_ref,
                 mask_ref, oa_ref, la_ref, ob_ref, lb_ref, oc_ref, lc_ref, on_ref):
    @pl.when((pl.program_id(0) == 0) & (pl.program_id(1) == 0))
    def _():
        for idx, m in enumerate(_band_masks()):
            mask_ref[idx] = m

    lane = lax.broadcasted_iota(jnp.int32, (BQ, LANES), 1)
    head0 = lane < HEAD_DIM
    scale = HEAD_DIM ** -0.5

    def block(q, k, v, valid):
        zero = jnp.zeros_like(q)
        qs = q * jnp.asarray(scale, q.dtype)
        qq = jnp.concatenate([jnp.where(head0, qs, zero), jnp.where(head0, zero, qs)], axis=0)
        sc = lax.dot_general(qq, k, (((1,), (1,)), ((), ())), preferred_element_type=F32)
        sc = jnp.where(jnp.concatenate([valid, valid], axis=0), sc, NEG_INF)
        m = jnp.max(sc, axis=-1, keepdims=True)
        p = jnp.exp(sc - m)
        den = jnp.sum(p, axis=-1, keepdims=True)
        o = jnp.dot(p.astype(BF16), v, preferred_element_type=F32) / den
        lse = m + jnp.log(den)
        return (jnp.where(head0, o[:BQ], o[BQ:]),
                jnp.where(head0, lse[:BQ], lse[BQ:]))


    def body_a(it, carry):
        for u in range(UNROLL):
            n = it * UNROLL + u
            r0 = pl.multiple_of(n * BQ, BQ)
            rp = pl.multiple_of(jnp.maximum(n - 1, 0) * BQ, BQ)
            q = qn_ref[0, pl.ds(r0, BQ), :]
            k = jnp.concatenate([kn_ref[0, pl.ds(rp, BQ), :], kn_ref[0, pl.ds(r0, BQ), :]], axis=0)
            v = jnp.concatenate([vn_ref[0, pl.ds(rp, BQ), :], vn_ref[0, pl.ds(r0, BQ), :]], axis=0)
            valid = mask_ref[jnp.where(n == 0, MASK_A_FIRST, MASK_A)] > 0.5
            o, l = block(q, k, v, valid)
            oa_ref[pl.ds(r0, BQ), :] = o
            la_ref[pl.ds(r0, BQ), :] = l
        return carry

    lax.fori_loop(0, SEQ // BQ // UNROLL, body_a, 0)

    sub = N_RES // 4

    def body_b(r4, carry):
        starts = [pl.multiple_of((4 * a + r4) * N_RES, N_RES) for a in range(4)]
        for n in range(N_RES // sub):
            c0 = n * sub
            cp = max(n - 1, 0) * sub
            q = jnp.concatenate([qp_ref[0, pl.ds(st + c0, sub), :] for st in starts], axis=0)
            k = jnp.concatenate([kp_ref[0, pl.ds(st + cp, sub), :] for st in starts]
                                + [kp_ref[0, pl.ds(st + c0, sub), :] for st in starts], axis=0)
            v = jnp.concatenate([vp_ref[0, pl.ds(st + cp, sub), :] for st in starts]
                                + [vp_ref[0, pl.ds(st + c0, sub), :] for st in starts], axis=0)
            valid = mask_ref[MASK_B_FIRST if n == 0 else MASK_B] > 0.5
            o, l = block(q, k, v, valid)
            for a, st in enumerate(starts):
                ob_ref[pl.ds(st + c0, sub), :] = o[a * sub:(a + 1) * sub]
                lb_ref[pl.ds(st + c0, sub), :] = l[a * sub:(a + 1) * sub]
        return carry

    lax.fori_loop(0, 4, body_b, 0)

    def body_c(it, carry):
        for u in range(UNROLL):
            r0 = pl.multiple_of((it * UNROLL + u) * N_RES, N_RES)
            q = qp_ref[0, pl.ds(r0, N_RES), :]
            k = kp_ref[0, pl.ds(r0, N_RES), :]
            v = vp_ref[0, pl.ds(r0, N_RES), :]
            valid = mask_ref[MASK_C, :, 0:BQ] > 0.5
            o, l = block(q, k, v, valid)
            oc_ref[pl.ds(r0, N_RES), :] = o
            lc_ref[pl.ds(r0, N_RES), :] = l
        return carry

    lax.fori_loop(0, DIL // UNROLL, body_c, 0)

    for res in range(DIL):
        nat = pl.ds(res, N_RES, stride=DIL)
        blk = pl.ds(res * N_RES, N_RES)
        la, lb, lc = la_ref[nat, :], lb_ref[blk, :], lc_ref[blk, :]
        m = jnp.maximum(jnp.maximum(la, lb), lc)
        ea, eb, ec = jnp.exp(la - m), jnp.exp(lb - m), jnp.exp(lc - m)
        out = (ea * oa_ref[nat, :] + eb * ob_ref[blk, :] + ec * oc_ref[blk, :]) / (ea + eb + ec)
        on_ref[nat, :] = out
    o_ref[0] = on_ref[...].astype(BF16)


def _attention(qn, kn, vn, qp, kp, vp):
    b = qn.shape[0]
    spec = pl.BlockSpec((1, SEQ, LANES), lambda i, j: (i, 0, j))
    buf = pltpu.VMEM((SEQ, LANES), F32)
    return pl.pallas_call(
        _attn_kernel,
        grid=(b, D_ATTN // LANES),
        in_specs=[spec] * 6,
        out_specs=spec,
        out_shape=jax.ShapeDtypeStruct((b, SEQ, D_ATTN), BF16),
        scratch_shapes=[pltpu.VMEM((5, BQ, 2 * BQ), F32)] + [buf] * 7,
        compiler_params=_params(("arbitrary", "arbitrary")),
        name="dilated_attention",
    )(qn, kn, vn, qp, kp, vp)


META_E1, META_E2, META_R1, META_R2, META_W1, META_W2 = range(6)


def _outproj_router_kernel(conv_ref, attn_ref, x_ref, wo1_ref, wo2_ref, g_ref, rt_ref,
                           x2_ref, hn_ref, meta_ref, cnt_ref,
                           tri_ref, rhi_ref, rlo_ref, run_ref):
    @pl.when(pl.program_id(0) == 0)
    def _():
        row = lax.broadcasted_iota(jnp.int32, (TM3, TM3), 0)
        col = lax.broadcasted_iota(jnp.int32, (TM3, TM3), 1)
        tri_ref[...] = jnp.where(col < row, 1.0, 0.0).astype(BF16)
        rt = rt_ref[...]
        hi = rt.astype(BF16)
        rhi_ref[...] = hi
        rlo_ref[...] = (rt - hi.astype(F32)).astype(BF16)
        run_ref[...] = jnp.zeros_like(run_ref)

    mixed = (jnp.dot(conv_ref[...], wo1_ref[...], preferred_element_type=F32)
             + jnp.dot(attn_ref[...], wo2_ref[...], preferred_element_type=F32))
    x2 = x_ref[...] + mixed
    x2_ref[...] = x2
    hn = x2 * lax.rsqrt(jnp.mean(x2 * x2, axis=-1, keepdims=True) + EPS) * g_ref[...]
    _store_rows(hn_ref, hn)

    hi = hn.astype(BF16)
    lo = (hn - hi.astype(F32)).astype(BF16)
    logits = (jnp.dot(hi, rhi_ref[...], preferred_element_type=F32)
              + jnp.dot(hi, rlo_ref[...], preferred_element_type=F32)
              + jnp.dot(lo, rhi_ref[...], preferred_element_type=F32))

    lane = lax.broadcasted_iota(jnp.int32, (TM3, LANES), 1).astype(F32)
    big = jnp.float32(1e9)
    low = jnp.float32(-3e38)
    rmax = lambda v: jnp.max(v, axis=-1, keepdims=True)
    rmin = lambda v: jnp.min(v, axis=-1, keepdims=True)
    rsum = lambda v: jnp.sum(v, axis=-1, keepdims=True)

    is_g = lane < N_GROUPS
    gmax = rmax(jnp.where(is_g, logits, low))
    gidx = rmin(jnp.where(is_g & (logits == gmax), lane, big))
    p_top = 1.0 / rsum(jnp.where(is_g, jnp.exp(logits - gmax), 0.0))

    first = ROUTER_LANE0 + EXPERTS_PER_GROUP * gidx
    sel = (lane >= first) & (lane < first + EXPERTS_PER_GROUP)
    e1 = rmax(jnp.where(sel, logits, low))
    i1 = rmin(jnp.where(sel & (logits == e1), lane, big))
    sel2 = sel & (lane != i1)
    e2 = rmax(jnp.where(sel2, logits, low))
    i2 = rmin(jnp.where(sel2 & (logits == e2), lane, big))
    t = jnp.exp(e2 - e1)
    w1 = p_top / (1.0 + t)
    w2 = p_top * t / (1.0 + t)

    hit1 = lane == i1
    hit2 = lane == i2
    onehot = jnp.where(hit1 | hit2, 1.0, 0.0).astype(F32)
    before = jnp.dot(tri_ref[...], onehot.astype(BF16), preferred_element_type=F32) + run_ref[...]
    r1 = rsum(jnp.where(hit1, before, 0.0))
    r2 = rsum(jnp.where(hit2, before, 0.0))
    run_ref[...] = run_ref[...] + jnp.sum(onehot, axis=0, keepdims=True)
    cnt_ref[...] = run_ref[...]

    cols = {META_E1: i1 - ROUTER_LANE0, META_E2: i2 - ROUTER_LANE0,
            META_R1: r1, META_R2: r2, META_W1: w1, META_W2: w2}
    meta = jnp.zeros((TM3, LANES), F32)
    for idx, val in cols.items():
        meta = jnp.where(lane == idx, val, meta)
    meta_ref[...] = meta


def _outproj_router(conv, attn, x, wo1, wo2, g, rt):
    t = x.shape[0]
    const = lambda *shape: pl.BlockSpec(shape, lambda i: (0,) * len(shape))
    return pl.pallas_call(
        _outproj_router_kernel,
        grid=(t // TM3,),
        in_specs=[
            pl.BlockSpec((TM3, D_CONV), lambda i: (i, 0)),
            pl.BlockSpec((TM3, D_ATTN), lambda i: (i, 0)),
            pl.BlockSpec((TM3, D_MODEL), lambda i: (i, 0)),
            const(D_CONV, D_MODEL),
            const(D_ATTN, D_MODEL),
            const(1, D_MODEL),
            const(D_MODEL, LANES),
        ],
        out_specs=[
            pl.BlockSpec((TM3, D_MODEL), lambda i: (i, 0)),
            pl.BlockSpec((TM3 * ROW_TILES, LANES), lambda i: (i, 0)),
            pl.BlockSpec((TM3, LANES), lambda i: (i, 0)),
            const(1, LANES),
        ],
        out_shape=[
            jax.ShapeDtypeStruct((t, D_MODEL), F32),
            jax.ShapeDtypeStruct((t * ROW_TILES, LANES), F32),
            jax.ShapeDtypeStruct((t, LANES), F32),
            jax.ShapeDtypeStruct((1, LANES), F32),
        ],
        scratch_shapes=[
            pltpu.VMEM((TM3, TM3), BF16),
            pltpu.VMEM((D_MODEL, LANES), BF16),
            pltpu.VMEM((D_MODEL, LANES), BF16),
            pltpu.VMEM((1, LANES), F32),
        ],
        compiler_params=_params(("arbitrary",)),
        name="outproj_router",
    )(conv, attn, x, wo1, wo2, g, rt)


def _row(ref, r):
    return ref.at[pl.ds(pl.multiple_of(r * ROW_TILES, ROW_TILES), ROW_TILES), :]


def _rows(ref, r0, n):
    return ref.at[pl.ds(pl.multiple_of(r0 * ROW_TILES, ROW_TILES), n * ROW_TILES), :]


def _load_rows(ref, r0, n):
    return jnp.concatenate(
        [ref[pl.ds(r0 * ROW_TILES + c, n, stride=ROW_TILES), :] for c in range(ROW_TILES)], axis=1)


def _store_rows(ref, val):
    n = val.shape[0]
    for c in range(ROW_TILES):
        ref[pl.ds(c, n, stride=ROW_TILES), :] = val[:, c * LANES:(c + 1) * LANES]


def _dispatch_kernel(dest_ref, cnt_ref, pstart_ref, pcnt_ref, nused_ref, hn_ref, xs_ref,
                     zero_ref, sem, zsem):
    i = pl.program_id(0)
    base = i * (2 * TM4)

    def copy_row(t, carry):
        for k in range(2):
            d = dest_ref[base + 2 * t + k]
            pltpu.make_async_copy(_row(hn_ref, t), _row(xs_ref, d), sem).start()
        return carry

    lax.fori_loop(0, TM4, copy_row, 0)

    @pl.when(i == 0)
    def _():
        zero_ref[...] = jnp.zeros_like(zero_ref)

        def per_expert(e, total):
            n_pad = pcnt_ref[e] - cnt_ref[e]
            first = pstart_ref[e] + cnt_ref[e]

            def fill(r, carry):
                pltpu.make_async_copy(_row(zero_ref, 0), _row(xs_ref, first + r), zsem).start()
                return carry

            lax.fori_loop(0, n_pad, fill, 0)
            return total + n_pad

        total = lax.fori_loop(0, N_EXPERTS, per_expert, 0)

        def drain(r, carry):
            pltpu.make_async_copy(_row(zero_ref, 0), _row(xs_ref, 0), zsem).wait()
            return carry

        lax.fori_loop(0, total, drain, 0)

        n_blk = xs_ref.shape[0] // (BLK * ROW_TILES)
        tail = lambda blk: pltpu.make_async_copy(zero_ref, _rows(xs_ref, blk * BLK, BLK), zsem)

        def fill_block(blk, carry):
            tail(blk).start()
            return carry

        def drain_block(blk, carry):
            tail(blk).wait()
            return carry

        lax.fori_loop(nused_ref[0], n_blk, fill_block, 0)
        lax.fori_loop(nused_ref[0], n_blk, drain_block, 0)

    for _ in range(2):
        pltpu.make_async_copy(hn_ref, _rows(xs_ref, 0, TM4), sem).wait()


def _dispatch(dest, counts, pstarts, pcounts, n_used, hn3, n_rows):
    t = hn3.shape[0] // ROW_TILES
    return pl.pallas_call(
        _dispatch_kernel,
        grid_spec=pltpu.PrefetchScalarGridSpec(
            num_scalar_prefetch=5,
            grid=(t // TM4,),
            in_specs=[pl.BlockSpec((TM4 * ROW_TILES, LANES), lambda i, *_: (i, 0))],
            out_specs=pl.BlockSpec(memory_space=pl.ANY),
            scratch_shapes=[
                pltpu.VMEM((BLK * ROW_TILES, LANES), F32),
                pltpu.SemaphoreType.DMA,
                pltpu.SemaphoreType.DMA,
            ],
        ),
        out_shape=jax.ShapeDtypeStruct((n_rows * ROW_TILES, LANES), F32),
        compiler_params=_params(("arbitrary",)),
        name="moe_dispatch",
    )(dest, counts, pstarts, pcounts, n_used, hn3)


def _ffn_kernel(bexp_ref, nused_ref, x_ref, wg_ref, wu_ref, wd_ref, y_ref):
    b = pl.program_id(0)

    @pl.when(b < nused_ref[0])
    def _():
        xb = _load_rows(x_ref, 0, BLK).astype(BF16)
        g = jnp.dot(xb, wg_ref[0], preferred_element_type=F32)
        u = jnp.dot(xb, wu_ref[0], preferred_element_type=F32)
        h = (g * jax.nn.sigmoid(g) * u).astype(BF16)
        y = jnp.dot(h, wd_ref[0], preferred_element_type=F32)
        _store_rows(y_ref, y)

    @pl.when(b >= nused_ref[0])
    def _():
        y_ref[...] = jnp.zeros_like(y_ref)


def _expert_ffn(blk_expert, n_used, xs, wg, wu, wd):
    n_rows = xs.shape[0] // ROW_TILES
    return pl.pallas_call(
        _ffn_kernel,
        grid_spec=pltpu.PrefetchScalarGridSpec(
            num_scalar_prefetch=2,
            grid=(n_rows // BLK,),
            in_specs=[
                pl.BlockSpec((BLK * ROW_TILES, LANES),
                             lambda b, be, nu: (jnp.minimum(b, nu[0] - 1), 0)),
                pl.BlockSpec((1, D_MODEL, D_EXPERT), lambda b, be, nu: (be[b], 0, 0)),
                pl.BlockSpec((1, D_MODEL, D_EXPERT), lambda b, be, nu: (be[b], 0, 0)),
                pl.BlockSpec((1, D_EXPERT, D_MODEL), lambda b, be, nu: (be[b], 0, 0)),
            ],
            out_specs=pl.BlockSpec((BLK * ROW_TILES, LANES), lambda b, be, nu: (b, 0)),
        ),
        out_shape=jax.ShapeDtypeStruct((n_rows * ROW_TILES, LANES), F32),
        compiler_params=_params(("arbitrary",)),
        name="expert_ffn",
    )(blk_expert, n_used, xs, wg, wu, wd)


def _combine_kernel(dest_ref, x2_ref, meta_ref, g_ref, ys_ref, o_ref, ybuf_ref, sem):
    i = pl.program_id(0)
    base = i * (2 * TM6)

    def fetch_row(t, carry):
        for k in range(2):
            d = dest_ref[base + 2 * t + k]
            pltpu.make_async_copy(_row(ys_ref, d), _row(ybuf_ref, k * TM6 + t), sem).start()
        return carry

    lax.fori_loop(0, TM6, fetch_row, 0)
    pltpu.make_async_copy(_rows(ys_ref, 0, 2 * TM6), ybuf_ref, sem).wait()

    rows = lambda k: _load_rows(ybuf_ref, k * TM6, TM6)
    w1 = meta_ref[:, META_W1:META_W1 + 1]
    w2 = meta_ref[:, META_W2:META_W2 + 1]
    x = x2_ref[...] + w1 * rows(0) + w2 * rows(1)
    o_ref[...] = x * lax.rsqrt(jnp.mean(x * x, axis=-1, keepdims=True) + EPS) * g_ref[...]


def _combine(dest, x2, meta, g, ys):
    t = x2.shape[0]
    return pl.pallas_call(
        _combine_kernel,
        grid_spec=pltpu.PrefetchScalarGridSpec(
            num_scalar_prefetch=1,
            grid=(t // TM6,),
            in_specs=[
                pl.BlockSpec((TM6, D_MODEL), lambda i, *_: (i, 0)),
                pl.BlockSpec((TM6, LANES), lambda i, *_: (i, 0)),
                pl.BlockSpec((1, D_MODEL), lambda i, *_: (0, 0)),
                pl.BlockSpec(memory_space=pl.ANY),
            ],
            out_specs=pl.BlockSpec((TM6, D_MODEL), lambda i, *_: (i, 0)),
            scratch_shapes=[
                pltpu.VMEM((2 * TM6 * ROW_TILES, LANES), F32),
                pltpu.SemaphoreType.DMA,
            ],
        ),
        out_shape=jax.ShapeDtypeStruct((t, D_MODEL), F32),
        compiler_params=_params(("arbitrary",)),
        name="moe_combine",
    )(dest, x2, meta, g, ys)


def _layer(x, norm_mix_g, w_in, conv_dw_w, conv_dw_b, conv_ln_g, conv_ln_b, w_out,
           norm_ffn_g, router_group, router_expert, w_gate, w_up, w_down, out_g):
    b, s, d = x.shape
    assert s == SEQ and d == D_MODEL
    t = b * s
    row = lambda v: v.reshape(1, -1).astype(F32)

    w_in_b = w_in.astype(BF16)
    conv, qn, kn, vn, qp, kp, vp = _inproj(
        x, row(norm_mix_g), w_in_b[:, :2 * D_CONV], w_in_b[:, 2 * D_CONV:],
        conv_dw_w.astype(F32), row(conv_dw_b), row(conv_ln_g), row(conv_ln_b))
    flat = lambda a: a.reshape(b, SEQ, D_ATTN)
    attn = _attention(qn, kn, vn, flat(qp), flat(kp), flat(vp))

    w_out_b = w_out.astype(BF16)
    router = jnp.concatenate(
        [router_group, router_expert.reshape(D_MODEL, N_EXPERTS),
         jnp.zeros((D_MODEL, LANES - N_GROUPS - N_EXPERTS), F32)], axis=1)
    x2, hn3, meta, cnt = _outproj_router(
        conv.reshape(t, D_CONV), attn.reshape(t, D_ATTN), x.reshape(t, D_MODEL),
        w_out_b[:D_CONV], w_out_b[D_CONV:], row(norm_ffn_g), router)

    counts = cnt[0, ROUTER_LANE0:ROUTER_LANE0 + N_EXPERTS].astype(jnp.int32)
    pcounts = (counts + BLK - 1) // BLK * BLK
    pends = jnp.cumsum(pcounts)
    pstarts = pends - pcounts
    experts = meta[:, META_E1:META_E2 + 1].astype(jnp.int32)
    ranks = meta[:, META_R1:META_R2 + 1].astype(jnp.int32)
    onehot = experts[..., None] == jnp.arange(N_EXPERTS, dtype=jnp.int32)
    dest = (jnp.sum(jnp.where(onehot, pstarts, 0), axis=-1) + ranks).reshape(-1)
    n_blk = (2 * t) // BLK + N_EXPERTS
    blk_start = jnp.arange(n_blk, dtype=jnp.int32) * BLK
    blk_expert = jnp.minimum(jnp.sum(blk_start[:, None] >= pends[None, :], axis=1),
                             N_EXPERTS - 1).astype(jnp.int32)
    n_used = (pends[-1:] // BLK).astype(jnp.int32)

    xs = _dispatch(dest, counts, pstarts, pcounts, n_used, hn3, n_blk * BLK)
    ys = _expert_ffn(blk_expert, n_used, xs,
                     w_gate.astype(BF16), w_up.astype(BF16), w_down.astype(BF16))
    out = _combine(dest, x2, meta, row(out_g), ys)
    return out.reshape(b, s, d)


def kernel(x, norm_mix_g, w_in, conv_dw_w, conv_dw_b, conv_ln_g, conv_ln_b, w_out, norm_ffn_g,
           router_group, router_expert, expert_w_gate, expert_w_up, expert_w_down, norm_final_g):
    depth = w_in.shape[0]
    assert depth == 1
    return _layer(x, norm_mix_g[0], w_in[0], conv_dw_w[0], conv_dw_b[0], conv_ln_g[0],
                  conv_ln_b[0], w_out[0], norm_ffn_g[0], router_group[0], router_expert[0],
                  expert_w_gate[0], expert_w_up[0], expert_w_down[0], norm_final_g)
```
